```python
import math
import jax
import jax.numpy as jnp
from jax import lax
import numpy as np

D_MODEL = 1024
BATCH = 8
SEQ = 2048
DEPTH = 4
DEC_BATCH = 32
DEC_SEQ = 1
PAST_LEN = 8192
PAGE_SIZE = 128

RET_HEADS = 4
RET_DK = 128
RET_DV = 128
RET_CHUNK = 128
ROPE_BASE = 10000.0
CONV_CH = 512
CONV_WIDTH = 31
ATT_GROUPS = ((128, 1), (512, 4), (2048, 16))
N_GROUPS = 3
ATT_HEADS = 8
ATT_HD = 64
ATT_W = ATT_HEADS * ATT_HD
ATT_BLK = 128
N_BUCKETS = 32
MAX_DIST = 2048
N_BRANCH = 3
D_FF = 2816
N_EXPERTS = 8
TOP_K = 2
D_FF_E = 3584
LN_EPS = 1e-5
ALPHA = (2 * DEPTH) ** 0.25
BETA = (8 * DEPTH) ** -0.25
N_DENSE = (DEPTH + 1) // 2
N_MOE = DEPTH // 2
SPLIT_SIZES = (RET_HEADS * RET_DK, RET_HEADS * RET_DK, RET_HEADS * RET_DV, RET_HEADS * RET_DV, 2 * CONV_CH) + (ATT_W,) * (3 * N_GROUPS) + (D_MODEL,) * N_BRANCH
SPLIT_IDX = tuple(int(i) for i in np.cumsum(SPLIT_SIZES)[:-1])
N_IN = sum(SPLIT_SIZES)

kernel_name = 'hybrid_retention_conformer_dilated_attn_step'


def layer_norm(x, g, b):
    xf = x.astype(jnp.float32)
    mu = xf.mean(-1, keepdims=True)
    var = jnp.square(xf - mu).mean(-1, keepdims=True)
    return ((xf - mu) * lax.rsqrt(var + LN_EPS) * g.astype(jnp.float32) + b.astype(jnp.float32)).astype(x.dtype)


def rotary(x, pos):
    half = x.shape[-1] // 2
    inv = ROPE_BASE ** (-jnp.arange(half, dtype=jnp.float32) / half)
    ang = pos.astype(jnp.float32)[:, None] * inv[None, :]
    cos = jnp.cos(ang)[None, :, None, :]
    sin = jnp.sin(ang)[None, :, None, :]
    x1, x2 = x[..., :half], x[..., half:]
    return jnp.concatenate([x1 * cos - x2 * sin, x1 * sin + x2 * cos], axis=-1)


def retention_scan(q, k, v, state, chunk):
    B, L, H, dk = q.shape
    dv = v.shape[-1]
    n = L // chunk
    lg = jnp.log1p(-jnp.exp2(-5.0 - jnp.arange(H, dtype=jnp.float32)))
    i = jnp.arange(chunk, dtype=jnp.float32)
    diff = i[:, None] - i[None, :]
    intra = jnp.where(diff >= 0, jnp.exp(jnp.maximum(diff, 0.0)[None] * lg[:, None, None]), 0.0)
    q_dec = jnp.exp((i + 1.0)[:, None] * lg[None, :])[None, :, :, None]
    k_dec = jnp.exp((chunk - 1.0 - i)[:, None] * lg[None, :])[None, :, :, None]
    c_dec = jnp.exp(chunk * lg)[None, :, None, None]

    def chunks(t):
        return t.reshape(B, n, chunk, H, t.shape[-1]).swapaxes(0, 1)

    def step(R, inp):
        qc, kc, vc = inp
        s = jnp.einsum('bihd,bjhd->bhij', qc, kc) * intra
        o = jnp.einsum('bhij,bjhe->bihe', s, vc) + jnp.einsum('bihd,bhde->bihe', qc, R) * q_dec
        R = R * c_dec + jnp.einsum('bjhd,bjhe->bhde', kc * k_dec, vc)
        return R, o

    R, o = lax.scan(step, state, (chunks(q), chunks(k), chunks(v)))
    return o.swapaxes(0, 1).reshape(B, L, H, dv), R


def retention_branch(rq, rk, rv, rg, pos, state, chunk, w_o):
    B, L, _ = rq.shape
    q = rotary(rq.astype(jnp.float32).reshape(B, L, RET_HEADS, RET_DK), pos)
    k = rotary(rk.astype(jnp.float32).reshape(B, L, RET_HEADS, RET_DK), pos) * RET_DK ** -0.5
    v = rv.astype(jnp.float32).reshape(B, L, RET_HEADS, RET_DV)
    o, R = retention_scan(q, k, v, state.astype(jnp.float32), chunk)
    mu = o.mean(-1, keepdims=True)
    var = jnp.square(o - mu).mean(-1, keepdims=True)
    o = ((o - mu) * lax.rsqrt(var + LN_EPS)).reshape(B, L, RET_HEADS * RET_DV)
    y = (jax.nn.silu(rg.astype(jnp.float32)) * o).astype(rq.dtype)
    return y @ w_o, R


def conv_branch(cu, buf, w_dw, b_dw, ln_g, ln_b, w_o):
    a, gt = jnp.split(cu, 2, axis=-1)
    u = a * jax.nn.sigmoid(gt)
    ext = jnp.concatenate([buf.astype(u.dtype), u], axis=1)
    y = lax.conv_general_dilated(ext, w_dw.astype(u.dtype)[:, None, :], (1,), 'VALID',
                                 dimension_numbers=('NWC', 'WIO', 'NWC'),
                                 feature_group_count=CONV_CH) + b_dw
    y = jax.nn.silu(layer_norm(y, ln_g, ln_b))
    return y @ w_o, ext[:, ext.shape[1] - (CONV_WIDTH - 1):]


def t5_bucket(dist):
    exact = N_BUCKETS // 2
    large = exact + (jnp.log(jnp.maximum(dist, 1).astype(jnp.float32) / exact)
                     / math.log(MAX_DIST / exact) * (N_BUCKETS - exact)).astype(jnp.int32)
    return jnp.where(dist < exact, dist, jnp.minimum(large, N_BUCKETS - 1))


def softmax_stats(s):
    m = jnp.max(s, axis=-1, keepdims=True)
    p = jnp.exp(s - m)
    l = jnp.sum(p, axis=-1, keepdims=True)
    return p / l, (m + jnp.log(l))[..., 0]


def dilated_attn_prompt(q, k, v, dil, n_keys, bias):
    B, L, H, hd = q.shape
    n = L // dil
    nb = -(-n // ATT_BLK)
    npad = nb * ATT_BLK

    def to_blocks(t):
        t = t.astype(jnp.float32).reshape(B, n, dil, H, hd).swapaxes(1, 2)
        t = jnp.pad(t, ((0, 0), (0, 0), (0, npad - n), (0, 0), (0, 0)))
        return t.reshape(B, dil, nb, ATT_BLK, H, hd)

    def with_prev(t):
        prev = jnp.pad(t, ((0, 0), (0, 0), (1, 0), (0, 0), (0, 0), (0, 0)))[:, :, :nb]
        return jnp.concatenate([prev, t], axis=3)

    qb = to_blocks(q)
    kk = with_prev(to_blocks(k))
    vv = with_prev(to_blocks(v))
    s = jnp.einsum('brnqhd,brnkhd->brnhqk', qb, kk) * ATT_HD ** -0.5
    qi = jnp.arange(ATT_BLK)[:, None]
    ki = jnp.arange(2 * ATT_BLK)[None, :]
    delta = ATT_BLK + qi - ki
    band = (delta >= 0) & (delta <= n_keys)
    rel = bias[t5_bucket(jnp.maximum(delta, 0) * dil)]
    s = s + jnp.moveaxis(rel, -1, 0).astype(jnp.float32)
    exists = (jnp.arange(nb)[:, None] > 0) | (ki >= ATT_BLK)
    mask = band[None] & exists[:, None, :]
    s = jnp.where(mask[None, None, :, None], s, -jnp.inf)
    p, lse = softmax_stats(s)
    o = jnp.einsum('brnhqk,brnkhd->brnqhd', p, vv)
    o = o.reshape(B, dil, npad, H, hd)[:, :, :n].swapaxes(1, 2).reshape(B, L, H, hd)
    lse = lse.swapaxes(3, 4).reshape(B, dil, npad, H)[:, :, :n].swapaxes(1, 2).reshape(B, L, H)
    return o, lse


def dilated_attn_sample(q, k, v, k_buf, v_buf, dil, n_keys, bias):
    B, T, H, hd = q.shape
    Lb = k_buf.shape[1]
    kk = jnp.concatenate([k_buf.astype(k.dtype), k], axis=1)
    vv = jnp.concatenate([v_buf.astype(v.dtype), v], axis=1)
    steps = jnp.arange(n_keys + 1)
    idx = Lb + jnp.arange(T)[:, None] - steps[None, :] * dil
    valid = idx >= 0
    idx = jnp.maximum(idx, 0)
    kg = kk[:, idx].astype(jnp.float32)
    vg = vv[:, idx].astype(jnp.float32)
    s = jnp.einsum('bthd,btjhd->bhtj', q.astype(jnp.float32), kg) * ATT_HD ** -0.5
    s = s + bias[t5_bucket(steps * dil)].T.astype(jnp.float32)[None, :, None, :]
    s = jnp.where(valid[None, None], s, -jnp.inf)
    p, lse = softmax_stats(s)
    o = jnp.einsum('bhtj,btjhd->bthd', p, vg)
    return o, lse.swapaxes(1, 2), kk[:, T:], vv[:, T:]


def token_mixers(x, pos, chunk, ret_state, conv_buf, att_bufs, rel_bias, w_in, w_ret_o, w_conv_dw,
                 b_conv_dw, conv_ln_g, conv_ln_b, w_conv_o, w_att_o, w_out):
    B, L, _ = x.shape
    parts = jnp.split(x @ w_in, SPLIT_IDX, axis=-1)
    rq, rk, rv, rg, cu = parts[:5]
    att_in = parts[5:5 + 3 * N_GROUPS]
    gates = parts[5 + 3 * N_GROUPS:]
    y_ret, ret_new = retention_branch(rq, rk, rv, rg, pos, ret_state, chunk, w_ret_o)
    y_conv, conv_new = conv_branch(cu, conv_buf, w_conv_dw, b_conv_dw, conv_ln_g, conv_ln_b, w_conv_o)
    outs, lses, att_new = [], [], []
    for g, (win, dil) in enumerate(ATT_GROUPS):
        q, k, v = [t.reshape(B, L, ATT_HEADS, ATT_HD) for t in att_in[3 * g:3 * g + 3]]
        bias = rel_bias[:, g * ATT_HEADS:(g + 1) * ATT_HEADS]
        if att_bufs is None:
            o, lse = dilated_attn_prompt(q, k, v, dil, win // dil, bias)
            keep = min(win, L)
            att_new += [k[:, L - keep:], v[:, L - keep:]]
        else:
            o, lse, kn, vn = dilated_attn_sample(q, k, v, att_bufs[g][0], att_bufs[g][1], dil, win // dil, bias)
            att_new += [kn, vn]
        outs.append(o)
        lses.append(lse)
    wts = jax.nn.softmax(jnp.stack(lses), axis=0)
    o = jnp.einsum('gblh,gblhd->blhd', wts, jnp.stack(outs)).reshape(B, L, ATT_W).astype(x.dtype)
    y_att = o @ w_att_o
    merged = (jax.nn.sigmoid(gates[0]) * y_ret + jax.nn.sigmoid(gates[1]) * y_conv
              + jax.nn.sigmoid(gates[2]) * y_att)
    return merged @ w_out, ret_new, conv_new, att_new


def swiglu(x, wg, wu, wd):
    return (jax.nn.silu(x @ wg) * (x @ wu)) @ wd


def moe_swiglu(x, w_r, b_r, wg, wu, wd):
    shp = x.shape
    xt = x.reshape(-1, shp[-1])
    logits = (xt @ w_r).astype(jnp.float32) + b_r.astype(jnp.float32)
    top_v, top_i = lax.top_k(logits, TOP_K)
    gate = jnp.sum(jax.nn.one_hot(top_i, N_EXPERTS, dtype=jnp.float32)
                   * jax.nn.softmax(top_v, axis=-1)[..., None], axis=1).astype(x.dtype)
    out = jnp.zeros_like(xt)
    for e in range(N_EXPERTS):
        out = out + gate[:, e:e + 1] * swiglu(xt, wg[e], wu[e], wd[e])
    return out.reshape(shp)


def channel_mix(t, l, w_ffn_gate, w_ffn_up, w_ffn_down, w_router, b_router, w_moe_gate, w_moe_up, w_moe_down):
    j = l // 2
    if l % 2 == 0:
        return swiglu(t, w_ffn_gate[j], w_ffn_up[j], w_ffn_down[j])
    return moe_swiglu(t, w_router[j], b_router[j], w_moe_gate[j], w_moe_up[j], w_moe_down[j])


def setup_inputs(seed: int = 0) -> dict:
    key = jax.random.key(seed)
    ks = iter(jax.random.split(key, 48))

    def nrm(shape, scale):
        return jax.random.normal(next(ks), shape, jnp.float32) * scale

    def abuf(w):
        return nrm((DEPTH, DEC_BATCH, min(w, PAST_LEN), ATT_HEADS, ATT_HD), 1.0)

    inp = {}
    inp['x_prompt'] = nrm((BATCH, SEQ, D_MODEL), 1.0)
    inp['x_sample'] = nrm((DEC_BATCH, DEC_SEQ, D_MODEL), 1.0)
    inp['state_ret'] = nrm((DEPTH, DEC_BATCH, RET_HEADS, RET_DK, RET_DV), 0.1)
    inp['cache_conv'] = nrm((DEPTH, DEC_BATCH, CONV_WIDTH - 1, CONV_CH), 0.5)
    inp['cache_k_w128'] = abuf(128)
    inp['cache_v_w128'] = abuf(128)
    inp['cache_k_w512'] = abuf(512)
    inp['cache_v_w512'] = abuf(512)
    inp['cache_k_w2048'] = abuf(2048)
    inp['cache_v_w2048'] = abuf(2048)
    inp['rel_bias'] = nrm((N_BUCKETS, N_GROUPS * ATT_HEADS), 0.1)
    inp['w_in'] = nrm((DEPTH, D_MODEL, N_IN), D_MODEL ** -0.5)
    inp['w_ret_o'] = nrm((DEPTH, RET_HEADS * RET_DV, D_MODEL), (RET_HEADS * RET_DV) ** -0.5)
    inp['w_conv_dw'] = nrm((DEPTH, CONV_WIDTH, CONV_CH), CONV_WIDTH ** -0.5)
    inp['b_conv_dw'] = nrm((DEPTH, CONV_CH), 0.01)
    inp['conv_ln_g'] = 1.0 + nrm((DEPTH, CONV_CH), 0.01)
    inp['conv_ln_b'] = nrm((DEPTH, CONV_CH), 0.01)
    inp['w_conv_o'] = nrm((DEPTH, CONV_CH, D_MODEL), CONV_CH ** -0.5)
    inp['w_att_o'] = nrm((DEPTH, ATT_W, D_MODEL), ATT_W ** -0.5)
    inp['w_out'] = nrm((DEPTH, D_MODEL, D_MODEL), BETA * D_MODEL ** -0.5)
    inp['ln1_g'] = 1.0 + nrm((DEPTH, D_MODEL), 0.01)
    inp['ln1_b'] = nrm((DEPTH, D_MODEL), 0.01)
    inp['ln2_g'] = 1.0 + nrm((DEPTH, D_MODEL), 0.01)
    inp['ln2_b'] = nrm((DEPTH, D_MODEL), 0.01)
    inp['w_ffn_gate'] = nrm((N_DENSE, D_MODEL, D_FF), D_MODEL ** -0.5)
    inp['w_ffn_up'] = nrm((N_DENSE, D_MODEL, D_FF), D_MODEL ** -0.5)
    inp['w_ffn_down'] = nrm((N_DENSE, D_FF, D_MODEL), BETA * D_FF ** -0.5)
    inp['w_router'] = nrm((N_MOE, D_MODEL, N_EXPERTS), D_MODEL ** -0.5)
    inp['b_router'] = nrm((N_MOE, N_EXPERTS), 0.01)
    inp['w_moe_gate'] = nrm((N_MOE, N_EXPERTS, D_MODEL, D_FF_E), D_MODEL ** -0.5)
    inp['w_moe_up'] = nrm((N_MOE, N_EXPERTS, D_MODEL, D_FF_E), D_MODEL ** -0.5)
    inp['w_moe_down'] = nrm((N_MOE, N_EXPERTS, D_FF_E, D_MODEL), BETA * D_FF_E ** -0.5)
    return inp


def reference(x_prompt, x_sample, state_ret, cache_conv, cache_k_w128, cache_v_w128, cache_k_w512,
              cache_v_w512, cache_k_w2048, cache_v_w2048, rel_bias, w_in, w_ret_o, w_conv_dw, b_conv_dw,
              conv_ln_g, conv_ln_b, w_conv_o, w_att_o, w_out, ln1_g, ln1_b, ln2_g, ln2_b, w_ffn_gate,
              w_ffn_up, w_ffn_down, w_router, b_router, w_moe_gate, w_moe_up, w_moe_down):
    xp, xs = x_prompt, x_sample
    Bp, Lp, _ = xp.shape
    Ts = xs.shape[1]
    pos_p = jnp.arange(Lp)
    pos_s = PAST_LEN + jnp.arange(Ts)
    ret_p, ret_s, conv_p, conv_s, att_p, att_s = [], [], [], [], [], []
    ffn = (w_ffn_gate, w_ffn_up, w_ffn_down, w_router, b_router, w_moe_gate, w_moe_up, w_moe_down)
    for l in range(DEPTH):
        lw = (rel_bias, w_in[l], w_ret_o[l], w_conv_dw[l], b_conv_dw[l], conv_ln_g[l], conv_ln_b[l],
              w_conv_o[l], w_att_o[l], w_out[l])
        yp, r, c, a = token_mixers(xp, pos_p, RET_CHUNK,
                                   jnp.zeros((Bp, RET_HEADS, RET_DK, RET_DV), jnp.float32),
                                   jnp.zeros((Bp, CONV_WIDTH - 1, CONV_CH), xp.dtype), None, *lw)
        ret_p.append(r)
        conv_p.append(c)
        att_p.append(a)
        bufs = ((cache_k_w128[l], cache_v_w128[l]), (cache_k_w512[l], cache_v_w512[l]),
                (cache_k_w2048[l], cache_v_w2048[l]))
        ys, r, c, a = token_mixers(xs, pos_s, Ts, state_ret[l], cache_conv[l], bufs, *lw)
        ret_s.append(r)
        conv_s.append(c)
        att_s.append(a)
        xp = layer_norm(ALPHA * xp + yp, ln1_g[l], ln1_b[l])
        xs = layer_norm(ALPHA * xs + ys, ln1_g[l], ln1_b[l])
        xp = layer_norm(ALPHA * xp + channel_mix(xp, l, *ffn), ln2_g[l], ln2_b[l])
        xs = layer_norm(ALPHA * xs + channel_mix(xs, l, *ffn), ln2_g[l], ln2_b[l])
    new_ret_p = jnp.stack(ret_p)
    new_ret_s = jnp.stack(ret_s)
    new_conv_p = jnp.stack(conv_p)
    new_conv_s = jnp.stack(conv_s)
    k128_p = jnp.stack([a[0] for a in att_p])
    k128_s = jnp.stack([a[0] for a in att_s])
    v128_p = jnp.stack([a[1] for a in att_p])
    v128_s = jnp.stack([a[1] for a in att_s])
    k512_p = jnp.stack([a[2] for a in att_p])
    k512_s = jnp.stack([a[2] for a in att_s])
    v512_p = jnp.stack([a[3] for a in att_p])
    v512_s = jnp.stack([a[3] for a in att_s])
    k2048_p = jnp.stack([a[4] for a in att_p])
    k2048_s = jnp.stack([a[4] for a in att_s])
    v2048_p = jnp.stack([a[5] for a in att_p])
    v2048_s = jnp.stack([a[5] for a in att_s])
    return (xp, xs, new_ret_p, new_ret_s, new_conv_p, new_conv_s, k128_p, k128_s, v128_p, v128_s,
            k512_p, k512_s, v512_p, v512_s, k2048_p, k2048_s, v2048_p, v2048_s)
```

```python
import functools
import math

import jax
import jax.numpy as jnp
import numpy as np
from jax import lax
from jax.experimental import pallas as pl
from jax.experimental.pallas import tpu as pltpu

F32 = jnp.float32
BF16 = jnp.bfloat16
HI = lax.Precision.HIGHEST

D_MODEL = 1024
DEPTH = 4
PAST_LEN = 8192
RET_HEADS = 4
RET_DK = 128
RET_DV = 128
RET_CHUNK = 128
ROPE_BASE = 10000.0
CONV_CH = 512
CONV_WIDTH = 31
ATT_GROUPS = ((128, 1), (512, 4), (2048, 16))
N_GROUPS = 3
ATT_HEADS = 8
ATT_HD = 64
ATT_W = ATT_HEADS * ATT_HD
ATT_BLK = 128
N_BUCKETS = 32
MAX_DIST = 2048
N_EXPERTS = 8
D_FF = 2816
D_FF_E = 3584
LN_EPS = 1e-5
ALPHA = (2 * DEPTH) ** 0.25

COL_RET = 0
COL_CONV = 4 * RET_HEADS * RET_DK
COL_ATT = COL_CONV + 2 * CONV_CH
COL_GATE = COL_ATT + 3 * N_GROUPS * ATT_W
N_MAIN = COL_GATE
N_GATE = 3 * D_MODEL

ROW_ALIGN = 1536
TM_PROJ = 1536
TN_PROJ = 512
TM_ROW = 512
TM_FFN = 768
TF_FFN = 256
TR_MOE = 1024
TF_MOE = 512
VMEM_LIMIT = 56 * 1024 * 1024


def _cparams(sem):
    return pltpu.CompilerParams(dimension_semantics=sem, vmem_limit_bytes=VMEM_LIMIT)


def _ln_rows(x, g, b):
    mu = jnp.mean(x, axis=-1, keepdims=True)
    xc = x - mu
    var = jnp.mean(xc * xc, axis=-1, keepdims=True)
    return xc * lax.rsqrt(var + LN_EPS) * g + b


def _sigmoid(x):
    return 1.0 / (1.0 + jnp.exp(-x))


def _silu(x):
    return x * _sigmoid(x)


def _proj_kernel(x_ref, w_ref, o_ref, wb_ref):
    @pl.when(pl.program_id(1) == 0)
    def _():
        wb_ref[...] = w_ref[...].astype(BF16)

    o_ref[...] = jnp.dot(x_ref[...], wb_ref[...], preferred_element_type=F32).astype(o_ref.dtype)


def _proj(xb, w_in, l, col0, ncols):
    nt = xb.shape[0]
    tm, tn = TM_PROJ, TN_PROJ
    cb0 = col0 // tn
    return pl.pallas_call(
        _proj_kernel,
        grid=(ncols // tn, nt // tm),
        in_specs=[pl.BlockSpec((tm, D_MODEL), lambda n, i: (i, 0)),
                  pl.BlockSpec((None, D_MODEL, tn), lambda n, i: (l, 0, cb0 + n))],
        out_specs=pl.BlockSpec((tm, tn), lambda n, i: (i, n)),
        out_shape=jax.ShapeDtypeStruct((nt, ncols), BF16),
        scratch_shapes=[pltpu.VMEM((D_MODEL, tn), BF16)],
        compiler_params=_cparams(("arbitrary", "arbitrary")),
        name="proj",
    )(xb, w_in)


def _rot(x, c2, s2):
    return x * c2 + pltpu.roll(x, RET_DK // 2, 1) * s2


def _ret_prompt_kernel(cdec, h_ref, cos_ref, sin_ref, intra_ref, qdec_ref, kdec_ref, y_ref, st_ref, r_ref):
    nchunk = h_ref.shape[0] // RET_CHUNK
    r_ref[...] = jnp.zeros_like(r_ref)

    def chunk(c, carry):
        r0 = pl.multiple_of(c * RET_CHUNK, RET_CHUNK)
        rows = pl.ds(r0, RET_CHUNK)
        c2 = cos_ref[rows, :]
        s2 = sin_ref[rows, :]
        for h in range(RET_HEADS):
            q = _rot(h_ref[rows, h * RET_DK:(h + 1) * RET_DK].astype(F32), c2, s2)
            k = _rot(h_ref[rows, 512 + h * RET_DK:512 + (h + 1) * RET_DK].astype(F32), c2, s2) * (RET_DK ** -0.5)
            v = h_ref[rows, 1024 + h * RET_DV:1024 + (h + 1) * RET_DV]
            g = h_ref[rows, 1536 + h * RET_DV:1536 + (h + 1) * RET_DV].astype(F32)
            qb = q.astype(BF16)
            s = lax.dot_general(qb, k.astype(BF16), (((1,), (1,)), ((), ())),
                                preferred_element_type=F32) * intra_ref[h]
            r_old = r_ref[h]
            o = (jnp.dot(s.astype(BF16), v, preferred_element_type=F32)
                 + jnp.dot(qb, r_old.astype(BF16), preferred_element_type=F32) * qdec_ref[h])
            kd = (k * kdec_ref[h]).astype(BF16)
            r_ref[h] = r_old * cdec[h] + lax.dot_general(kd, v, (((0,), (0,)), ((), ())),
                                                          preferred_element_type=F32)
            mu = jnp.mean(o, axis=-1, keepdims=True)
            oc = o - mu
            var = jnp.mean(oc * oc, axis=-1, keepdims=True)
            y_ref[rows, h * RET_DV:(h + 1) * RET_DV] = (_silu(g) * (oc * lax.rsqrt(var + LN_EPS))).astype(y_ref.dtype)
        return carry

    lax.fori_loop(0, nchunk, chunk, 0)
    st_ref[...] = r_ref[...]


def _ret_consts(seq):
    lg = jnp.log1p(-jnp.exp2(-5.0 - jnp.arange(RET_HEADS, dtype=F32)))
    i = jnp.arange(RET_CHUNK, dtype=F32)
    diff = i[:, None] - i[None, :]
    intra = jnp.where(diff >= 0, jnp.exp(jnp.maximum(diff, 0.0)[None] * lg[:, None, None]), 0.0)
    qdec = jnp.broadcast_to(jnp.exp((i + 1.0)[None, :] * lg[:, None])[:, :, None], (RET_HEADS, RET_CHUNK, RET_DV))
    kdec = jnp.broadcast_to(jnp.exp((RET_CHUNK - 1.0 - i)[None, :] * lg[:, None])[:, :, None],
                            (RET_HEADS, RET_CHUNK, RET_DK))
    lg64 = np.log1p(-np.exp2(-5.0 - np.arange(RET_HEADS, dtype=np.float32))).astype(np.float32)
    cdec = tuple(float(np.exp(np.float32(RET_CHUNK) * lg64[h])) for h in range(RET_HEADS))
    gamma = tuple(float(np.exp(lg64[h])) for h in range(RET_HEADS))
    return intra.astype(F32), qdec.astype(F32), kdec.astype(F32), cdec, gamma


def _rope_tables(pos):
    half = RET_DK // 2
    inv = ROPE_BASE ** (-jnp.arange(half, dtype=F32) / half)
    ang = pos.astype(F32)[:, None] * inv[None, :]
    cos, sin = jnp.cos(ang), jnp.sin(ang)
    return jnp.concatenate([cos, cos], axis=-1), jnp.concatenate([-sin, sin], axis=-1)


def _ret_prompt(h_main, nb, seq, consts, ropes):
    nt = h_main.shape[0]
    intra, qdec, kdec, cdec, _ = consts
    cos2, sin2 = ropes
    full = lambda shape: pl.BlockSpec(shape, lambda b: (0,) * len(shape))
    return pl.pallas_call(
        functools.partial(_ret_prompt_kernel, cdec),
        grid=(nb,),
        in_specs=[pl.BlockSpec((seq, 4 * RET_HEADS * RET_DK), lambda b: (b, 0)),
                  full((seq, RET_DK)), full((seq, RET_DK)),
                  full((RET_HEADS, RET_CHUNK, RET_CHUNK)), full((RET_HEADS, RET_CHUNK, RET_DV)),
                  full((RET_HEADS, RET_CHUNK, RET_DK))],
        out_specs=[pl.BlockSpec((seq, RET_HEADS * RET_DV), lambda b: (b, 0)),
                   pl.BlockSpec((None, RET_HEADS, RET_DK, RET_DV), lambda b: (b, 0, 0, 0))],
        out_shape=[jax.ShapeDtypeStruct((nb * seq, RET_HEADS * RET_DV), BF16),
                   jax.ShapeDtypeStruct((nb, RET_HEADS, RET_DK, RET_DV), F32)],
        scratch_shapes=[pltpu.VMEM((RET_HEADS, RET_DK, RET_DV), F32)],
        compiler_params=_cparams(("arbitrary",)),
        name="ret_prompt",
    )(h_main, cos2, sin2, intra, qdec, kdec)


CONV_HALO = 32
CONV_ROWS = 64


def _conv_prompt_kernel(h_ref, w_ref, b_ref, g_ref, beta_ref, y_ref, tail_ref, u_ref):
    seq = h_ref.shape[0]
    u_ref[0:CONV_HALO, :] = jnp.zeros((CONV_HALO, CONV_CH), F32)

    def glu(c, carry):
        r0 = pl.multiple_of(c * 256, 256)
        a = h_ref[pl.ds(r0, 256), 0:CONV_CH].astype(F32)
        gt = h_ref[pl.ds(r0, 256), CONV_CH:2 * CONV_CH].astype(F32)
        u_ref[pl.ds(CONV_HALO + r0, 256), :] = a * _sigmoid(gt)
        return carry

    lax.fori_loop(0, seq // 256, glu, 0)
    tail_ref[...] = u_ref[CONV_HALO + seq - (CONV_WIDTH - 1):CONV_HALO + seq, :]

    off = CONV_HALO - (CONV_WIDTH - 1)

    def conv(c, carry):
        r0 = pl.multiple_of(c * CONV_ROWS, CONV_ROWS)
        win = u_ref[pl.ds(r0, CONV_ROWS + CONV_HALO), :]
        acc = jnp.zeros((CONV_ROWS, CONV_CH), F32)
        for j in range(CONV_WIDTH):
            acc = acc + win[off + j:off + j + CONV_ROWS, :] * w_ref[j:j + 1, :]
        y = _ln_rows(acc + b_ref[...], g_ref[...], beta_ref[...])
        y_ref[pl.ds(r0, CONV_ROWS), :] = _silu(y).astype(y_ref.dtype)
        return carry

    lax.fori_loop(0, seq // CONV_ROWS, conv, 0)


def _conv_prompt(h_main, nb, seq, w_dw, b_dw, ln_g, ln_b):
    nt = h_main.shape[0]
    full = lambda shape: pl.BlockSpec(shape, lambda b: (0,) * len(shape))
    return pl.pallas_call(
        _conv_prompt_kernel,
        grid=(nb,),
        in_specs=[pl.BlockSpec((seq, 2 * CONV_CH), lambda b: (b, COL_CONV // (2 * CONV_CH))),
                  full((CONV_WIDTH, CONV_CH)), full((1, CONV_CH)), full((1, CONV_CH)), full((1, CONV_CH))],
        out_specs=[pl.BlockSpec((seq, CONV_CH), lambda b: (b, 0)),
                   pl.BlockSpec((None, CONV_WIDTH - 1, CONV_CH), lambda b: (b, 0, 0))],
        out_shape=[jax.ShapeDtypeStruct((nb * seq, CONV_CH), BF16),
                   jax.ShapeDtypeStruct((nb, CONV_WIDTH - 1, CONV_CH), F32)],
        scratch_shapes=[pltpu.VMEM((seq + CONV_HALO, CONV_CH), F32)],
        compiler_params=_cparams(("arbitrary",)),
        name="conv_prompt",
    )(h_main, w_dw, b_dw.reshape(1, CONV_CH), ln_g.reshape(1, CONV_CH), ln_b.reshape(1, CONV_CH))


def _t5_bucket(dist):
    exact = N_BUCKETS // 2
    large = exact + (jnp.log(jnp.maximum(dist, 1).astype(F32) / exact)
                     / math.log(MAX_DIST / exact) * (N_BUCKETS - exact)).astype(jnp.int32)
    return jnp.where(dist < exact, dist, jnp.minimum(large, N_BUCKETS - 1))


def _att_prompt_kernel(g, has_prev, bias_ref, bko_ref, bkp_ref, q_ref, ko_ref, vo_ref, *rest):
    if has_prev:
        kp_ref, vp_ref, o_ref, lse_ref, bmo_ref, bmp_ref = rest
    else:
        o_ref, lse_ref, bmo_ref, bmp_ref = rest
    first = (pl.program_id(0) == 0) & (pl.program_id(1) == 0) & (pl.program_id(2) == 0)

    @pl.when(first)
    def _():
        qi = lax.broadcasted_iota(jnp.int32, (ATT_BLK, ATT_BLK), 0)
        ci = lax.broadcasted_iota(jnp.int32, (ATT_BLK, ATT_BLK), 1)
        bko = bko_ref[...]
        bkp = bkp_ref[...]
        for h in range(ATT_HEADS):
            to = jnp.zeros((ATT_BLK, ATT_BLK), F32)
            tp = jnp.zeros((ATT_BLK, ATT_BLK), F32)
            for b in range(N_BUCKETS):
                val = bias_ref[b, g * ATT_HEADS + h]
                to = jnp.where(bko == b, val, to)
                tp = jnp.where(bkp == b, val, tp)
            bmo_ref[h] = jnp.where(ci <= qi, to, -jnp.inf)
            bmp_ref[h] = jnp.where(ci >= qi, tp, -jnp.inf)

    blk = pl.program_id(2)
    q = q_ref[...]
    for h in range(ATT_HEADS):
        sl = slice(h * ATT_HD, (h + 1) * ATT_HD)
        qh = q[:, sl]
        so = lax.dot_general(qh, ko_ref[:, sl], (((1,), (1,)), ((), ())),
                             preferred_element_type=F32) * (ATT_HD ** -0.5) + bmo_ref[h]
        m = jnp.max(so, axis=-1, keepdims=True)
        if has_prev:
            sp = lax.dot_general(qh, kp_ref[:, sl], (((1,), (1,)), ((), ())),
                                 preferred_element_type=F32) * (ATT_HD ** -0.5) + bmp_ref[h]
            sp = jnp.where(blk > 0, sp, -jnp.inf)
            m = jnp.maximum(m, jnp.max(sp, axis=-1, keepdims=True))
        po = jnp.exp(so - m)
        l = jnp.sum(po, axis=-1, keepdims=True)
        acc = jnp.dot(po.astype(BF16), vo_ref[:, sl], preferred_element_type=F32)
        if has_prev:
            pp = jnp.exp(sp - m)
            l = l + jnp.sum(pp, axis=-1, keepdims=True)
            acc = acc + jnp.dot(pp.astype(BF16), vp_ref[:, sl], preferred_element_type=F32)
        o_ref[:, sl] = (acc / l).astype(o_ref.dtype)
        lse_ref[:, h:h + 1] = m + jnp.log(l)


def _att_prompt(h_main, rel_bias, g, nb, seq):
    nt = h_main.shape[0]
    win, dil = ATT_GROUPS[g]
    n = seq // dil
    nblk = n // ATT_BLK
    has_prev = nblk > 1
    hv = h_main.reshape(nt // dil, dil * N_MAIN)
    cpr = N_MAIN // ATT_W
    cq = COL_ATT // ATT_W + 3 * g
    qi = jnp.arange(ATT_BLK)[:, None]
    ci = jnp.arange(ATT_BLK)[None, :]
    bko = _t5_bucket(jnp.maximum(qi - ci, 0) * dil).astype(jnp.int32)
    bkp = _t5_bucket(jnp.maximum(ATT_BLK + qi - ci, 0) * dil).astype(jnp.int32)

    def own(c):
        return pl.BlockSpec((ATT_BLK, ATT_W), lambda b, r, i: (b * nblk + i, r * cpr + c))

    def prev(c):
        return pl.BlockSpec((ATT_BLK, ATT_W), lambda b, r, i: (b * nblk + jnp.maximum(i - 1, 0), r * cpr + c))

    tab = pl.BlockSpec((ATT_BLK, ATT_BLK), lambda b, r, i: (0, 0))
    in_specs = [pl.BlockSpec(memory_space=pltpu.SMEM), tab, tab, own(cq), own(cq + 1), own(cq + 2)]
    args = [rel_bias, bko, bkp, hv, hv, hv]
    if has_prev:
        in_specs += [prev(cq + 1), prev(cq + 2)]
        args += [hv, hv]
    o, lse = pl.pallas_call(
        functools.partial(_att_prompt_kernel, g, has_prev),
        grid=(nb, dil, nblk),
        in_specs=in_specs,
        out_specs=[pl.BlockSpec((ATT_BLK, ATT_W), lambda b, r, i: (b * nblk + i, r)),
                   pl.BlockSpec((None, None, ATT_BLK, ATT_HEADS), lambda b, r, i: (b, r, i, 0))],
        out_shape=[jax.ShapeDtypeStruct((nb * n, dil * ATT_W), BF16),
                   jax.ShapeDtypeStruct((nb, dil, n, ATT_HEADS), F32)],
        scratch_shapes=[pltpu.VMEM((ATT_HEADS, ATT_BLK, ATT_BLK), F32),
                        pltpu.VMEM((ATT_HEADS, ATT_BLK, ATT_BLK), F32)],
        compiler_params=_cparams(("arbitrary", "arbitrary", "arbitrary")),
        name=f"att_prompt_g{g}",
    )(*args)
    lse = jnp.swapaxes(lse, 1, 2).reshape(nb * seq, ATT_HEADS)
    return o.reshape(nb * seq, ATT_W), lse


SB = 8


def _ret_sample_kernel(gamma, h_ref, cos_ref, sin_ref, st_ref, y_ref, new_ref):
    c2 = cos_ref[...]
    s2 = sin_ref[...]
    eye = (lax.broadcasted_iota(jnp.int32, (RET_DK, RET_DK), 0)
           == lax.broadcasted_iota(jnp.int32, (RET_DK, RET_DK), 1)).astype(F32)
    ones = jnp.ones((RET_DK, RET_DV), F32)
    rowid = lax.broadcasted_iota(jnp.int32, (SB, RET_DV), 0)
    for h in range(RET_HEADS):
        q = _rot(h_ref[:, h * RET_DK:(h + 1) * RET_DK], c2, s2)
        k = _rot(h_ref[:, 512 + h * RET_DK:512 + (h + 1) * RET_DK], c2, s2) * (RET_DK ** -0.5)
        v = h_ref[:, 1024 + h * RET_DV:1024 + (h + 1) * RET_DV]
        g = h_ref[:, 1536 + h * RET_DV:1536 + (h + 1) * RET_DV]
        s = jnp.sum(q * k, axis=-1, keepdims=True)
        qr = jnp.zeros((SB, RET_DV), F32)
        for b in range(SB):
            r_old = st_ref[b, h]
            qr = jnp.where(rowid == b, jnp.dot(q, r_old, preferred_element_type=F32, precision=HI), qr)
            kcol = jnp.dot(eye * k[b:b + 1, :], ones, preferred_element_type=F32, precision=HI)
            new_ref[b, h] = r_old * gamma[h] + kcol * v[b:b + 1, :]
        o = s * v + qr * gamma[h]
        mu = jnp.mean(o, axis=-1, keepdims=True)
        oc = o - mu
        var = jnp.mean(oc * oc, axis=-1, keepdims=True)
        y_ref[:, h * RET_DV:(h + 1) * RET_DV] = _silu(g) * (oc * lax.rsqrt(var + LN_EPS))


def _ret_sample(hs, state, gamma, ropes, l):
    ns = hs.shape[0]
    cos2, sin2 = ropes
    return pl.pallas_call(
        functools.partial(_ret_sample_kernel, gamma),
        grid=(ns // SB,),
        in_specs=[pl.BlockSpec((SB, 4 * RET_HEADS * RET_DK), lambda i: (i, 0)),
                  pl.BlockSpec((1, RET_DK), lambda i: (0, 0)), pl.BlockSpec((1, RET_DK), lambda i: (0, 0)),
                  pl.BlockSpec((None, SB, RET_HEADS, RET_DK, RET_DV), lambda i: (l, i, 0, 0, 0))],
        out_specs=[pl.BlockSpec((SB, RET_HEADS * RET_DV), lambda i: (i, 0)),
                   pl.BlockSpec((SB, RET_HEADS, RET_DK, RET_DV), lambda i: (i, 0, 0, 0))],
        out_shape=[jax.ShapeDtypeStruct((ns, RET_HEADS * RET_DV), F32),
                   jax.ShapeDtypeStruct(state.shape[1:], F32)],
        compiler_params=_cparams(("arbitrary",)),
        name="ret_sample",
    )(hs, cos2, sin2, state)


def _conv_sample_kernel(h_ref, buf_ref, w_ref, b_ref, g_ref, beta_ref, y_ref, new_ref):
    a = h_ref[:, COL_CONV:COL_CONV + CONV_CH]
    gt = h_ref[:, COL_CONV + CONV_CH:COL_CONV + 2 * CONV_CH]
    u = a * _sigmoid(gt)
    acc = u * w_ref[CONV_WIDTH - 1:CONV_WIDTH, :]
    for j in range(CONV_WIDTH - 1):
        acc = acc + buf_ref[j] * w_ref[j:j + 1, :]
    y = _ln_rows(acc + b_ref[...], g_ref[...], beta_ref[...])
    y_ref[...] = _silu(y)
    for j in range(CONV_WIDTH - 2):
        new_ref[j] = buf_ref[j + 1]
    new_ref[CONV_WIDTH - 2] = u


def _conv_sample(hs, buf, w_dw, b_dw, ln_g, ln_b):
    ns = hs.shape[0]
    buf_t = jnp.swapaxes(buf, 0, 1)
    y, new_t = pl.pallas_call(
        _conv_sample_kernel,
        out_shape=[jax.ShapeDtypeStruct((ns, CONV_CH), F32), jax.ShapeDtypeStruct(buf_t.shape, F32)],
        compiler_params=pltpu.CompilerParams(vmem_limit_bytes=VMEM_LIMIT),
        name="conv_sample",
    )(hs, buf_t, w_dw, b_dw.reshape(1, CONV_CH), ln_g.reshape(1, CONV_CH), ln_b.reshape(1, CONV_CH))
    return y, jnp.swapaxes(new_t, 0, 1)


def _att_sample_kernel(q_ref, kn_ref, vn_ref, kc_ref, vc_ref, oh_ref, bias_ref, o_ref, lse_ref):
    q = q_ref[...] * (ATT_HD ** -0.5)
    btab = jnp.sum(oh_ref[...] * bias_ref[...][None], axis=-1, keepdims=True)
    sc = jnp.sum(kc_ref[...] * q[:, None], axis=-1, keepdims=True) + btab[0:ATT_BLK][None]
    sn = jnp.sum(kn_ref[...] * q, axis=-1, keepdims=True) + btab[ATT_BLK][None]
    m = jnp.maximum(jnp.max(sc, axis=1), sn)
    pc = jnp.exp(sc - m[:, None])
    pn = jnp.exp(sn - m)
    l = jnp.sum(pc, axis=1) + pn
    o_ref[...] = (jnp.sum(pc * vc_ref[...], axis=1) + pn * vn_ref[...]) / l
    lse_ref[...] = m + jnp.log(l)


def _att_sample(hs, kc, vc, rel_bias, g, l):
    ns = hs.shape[0]
    win, dil = ATT_GROUPS[g]
    lb = kc.shape[2]
    assert lb == ATT_BLK * dil, "cache shorter than the attention window is not supported"
    kv = kc.reshape(kc.shape[0], ns, ATT_BLK, dil, ATT_HEADS, ATT_HD)
    vv = vc.reshape(vc.shape[0], ns, ATT_BLK, dil, ATT_HEADS, ATT_HD)
    cq = COL_ATT + 3 * g * ATT_W
    q, kn, vn = [hs[:, cq + t * ATT_W:cq + (t + 1) * ATT_W].reshape(ns, ATT_HEADS, ATT_HD) for t in range(3)]
    dist = jnp.concatenate([(ATT_BLK - jnp.arange(ATT_BLK)) * dil, jnp.zeros((8,), jnp.int32)])
    onehot = jax.nn.one_hot(_t5_bucket(dist), N_BUCKETS, dtype=F32)[:, None, :]
    bias_t = rel_bias[:, g * ATT_HEADS:(g + 1) * ATT_HEADS].T
    row = pl.BlockSpec((SB, ATT_HEADS, ATT_HD), lambda i: (i, 0, 0))
    cache = pl.BlockSpec((None, SB, ATT_BLK, None, ATT_HEADS, ATT_HD), lambda i: (l, i, 0, 0, 0, 0))
    o, lse = pl.pallas_call(
        _att_sample_kernel,
        grid=(ns // SB,),
        in_specs=[row, row, row, cache, cache,
                  pl.BlockSpec(onehot.shape, lambda i: (0, 0, 0)),
                  pl.BlockSpec((ATT_HEADS, N_BUCKETS), lambda i: (0, 0))],
        out_specs=[row, pl.BlockSpec((SB, ATT_HEADS, 1), lambda i: (i, 0, 0))],
        out_shape=[jax.ShapeDtypeStruct((ns, ATT_HEADS, ATT_HD), F32),
                   jax.ShapeDtypeStruct((ns, ATT_HEADS, 1), F32)],
        compiler_params=_cparams(("arbitrary",)),
        name=f"att_sample_g{g}",
    )(q, kn, vn, kv, vv, onehot, bias_t)
    return o.reshape(ns, ATT_W), lse.reshape(ns, ATT_HEADS)


def _shift_kernel(n_layers, nt, *refs):
    caches = refs[:nt]
    news = refs[nt:2 * nt]
    outs = refs[2 * nt:3 * nt]
    sem = refs[3 * nt]
    copies = []
    for c, nw, o in zip(caches, news, outs):
        w = c.shape[2]
        for l in range(n_layers):
            copies.append(pltpu.make_async_copy(c.at[l, :, pl.ds(1, w - 1)], o.at[l, :, pl.ds(0, w - 1)], sem))
            copies.append(pltpu.make_async_copy(nw.at[l], o.at[l, :, pl.ds(w - 1, 1)], sem))
    for cp in copies:
        cp.start()
    for cp in copies:
        cp.wait()


def _shift_caches(caches, news):
    nt = len(caches)
    any_spec = pl.BlockSpec(memory_space=pl.ANY)
    return pl.pallas_call(
        functools.partial(_shift_kernel, caches[0].shape[0], nt),
        in_specs=[any_spec] * (2 * nt),
        out_specs=[any_spec] * nt,
        out_shape=[jax.ShapeDtypeStruct(c.shape, c.dtype) for c in caches],
        scratch_shapes=[pltpu.SemaphoreType.DMA(())],
        name="shift_caches",
    )(*caches, *news)


N_MIX = 6


def _merge_kernel(n_ptiles, x_ref, *refs):
    mix_p = refs[:N_MIX]
    mix_s = refs[N_MIX:2 * N_MIX]
    gt_ref, wr_ref, wc_ref, wa_ref, wo_ref, g_ref, b_ref, xo_ref, xb_ref, wrb, wcb, wab, wob = refs[2 * N_MIX:]

    @pl.when(pl.program_id(0) == 0)
    def _():
        wrb[...] = wr_ref[...].astype(BF16)
        wcb[...] = wc_ref[...].astype(BF16)
        wab[...] = wa_ref[...].astype(BF16)
        wob[...] = wo_ref[...].astype(BF16)

    is_sample = pl.program_id(0) >= n_ptiles
    yr, yc, o0, o1, o2, lse = [jnp.where(is_sample, s[...], p[...]) for p, s in zip(mix_p, mix_s)]
    head_lane = lax.broadcasted_iota(jnp.int32, (ATT_HEADS, ATT_W), 1) // ATT_HD
    segt = (head_lane == lax.broadcasted_iota(jnp.int32, (ATT_HEADS, ATT_W), 0)).astype(F32)
    l0 = lse[:, 0:ATT_HEADS]
    l1 = lse[:, ATT_HEADS:2 * ATT_HEADS]
    l2 = lse[:, 2 * ATT_HEADS:3 * ATT_HEADS]
    m = jnp.maximum(jnp.maximum(l0, l1), l2)
    e0 = jnp.exp(l0 - m)
    e1 = jnp.exp(l1 - m)
    e2 = jnp.exp(l2 - m)
    den = e0 + e1 + e2
    att = jnp.zeros(o0.shape, F32)
    for e, o in ((e0, o0), (e1, o1), (e2, o2)):
        att = att + jnp.dot(e / den, segt, preferred_element_type=F32, precision=HI) * o.astype(F32)
    y_ret = jnp.dot(yr, wrb[...], preferred_element_type=F32)
    y_conv = jnp.dot(yc, wcb[...], preferred_element_type=F32)
    y_att = jnp.dot(att.astype(BF16), wab[...], preferred_element_type=F32)
    merged = (_sigmoid(gt_ref[:, 0:D_MODEL].astype(F32)) * y_ret
              + _sigmoid(gt_ref[:, D_MODEL:2 * D_MODEL].astype(F32)) * y_conv
              + _sigmoid(gt_ref[:, 2 * D_MODEL:3 * D_MODEL].astype(F32)) * y_att)
    y = jnp.dot(merged.astype(BF16), wob[...], preferred_element_type=F32)
    xn = _ln_rows(ALPHA * x_ref[...] + y, g_ref[...], b_ref[...])
    xo_ref[...] = xn
    xb_ref[...] = xn.astype(BF16)


def _merge(x, mix_p, mix_s, gates, w_ret_o, w_conv_o, w_att_o, w_out, ln_g, ln_b, l):
    nt = x.shape[0]
    tm = TM_ROW
    n_ptiles = mix_p[0].shape[0] // tm
    row = lambda c: pl.BlockSpec((tm, c), lambda i: (i, 0))
    prow = lambda c: pl.BlockSpec((tm, c), lambda i: (jnp.minimum(i, n_ptiles - 1), 0))
    srow = lambda c: pl.BlockSpec((tm, c), lambda i: (jnp.maximum(i - n_ptiles, 0), 0))
    wsp = lambda k: pl.BlockSpec((None, k, D_MODEL), lambda i: (l, 0, 0))
    vec = pl.BlockSpec((None, 1, D_MODEL), lambda i: (l, 0, 0))
    return pl.pallas_call(
        functools.partial(_merge_kernel, n_ptiles),
        grid=(nt // tm,),
        in_specs=[row(D_MODEL)] + [prow(a.shape[1]) for a in mix_p] + [srow(a.shape[1]) for a in mix_s]
                 + [row(N_GATE), wsp(512), wsp(CONV_CH), wsp(ATT_W), wsp(D_MODEL), vec, vec],
        out_specs=[row(D_MODEL), row(D_MODEL)],
        out_shape=[jax.ShapeDtypeStruct((nt, D_MODEL), F32), jax.ShapeDtypeStruct((nt, D_MODEL), BF16)],
        scratch_shapes=[pltpu.VMEM((512, D_MODEL), BF16), pltpu.VMEM((CONV_CH, D_MODEL), BF16),
                        pltpu.VMEM((ATT_W, D_MODEL), BF16), pltpu.VMEM((D_MODEL, D_MODEL), BF16)],
        compiler_params=_cparams(("arbitrary",)),
        name="merge",
    )(x, *mix_p, *mix_s, gates, w_ret_o, w_conv_o, w_att_o, w_out,
      ln_g.reshape(DEPTH, 1, D_MODEL), ln_b.reshape(DEPTH, 1, D_MODEL))


def _ffn_kernel(x_ref, xb_ref, wg_ref, wu_ref, wd_ref, g_ref, b_ref, xo_ref, xbo_ref, wgb, wub, wdb, acc_ref):
    i = pl.program_id(0)
    f = pl.program_id(1)

    @pl.when(i == 0)
    def _():
        wgb[f] = wg_ref[...].astype(BF16)
        wub[f] = wu_ref[...].astype(BF16)
        wdb[f] = wd_ref[...].astype(BF16)

    @pl.when(f == 0)
    def _():
        acc_ref[...] = jnp.zeros_like(acc_ref)

    xb = xb_ref[...]
    hg = jnp.dot(xb, wgb[f], preferred_element_type=F32)
    hu = jnp.dot(xb, wub[f], preferred_element_type=F32)
    acc_ref[...] += jnp.dot((_silu(hg) * hu).astype(BF16), wdb[f], preferred_element_type=F32)

    @pl.when(f == pl.num_programs(1) - 1)
    def _():
        xn = _ln_rows(ALPHA * x_ref[...] + acc_ref[...], g_ref[...], b_ref[...])
        xo_ref[...] = xn
        xbo_ref[...] = xn.astype(BF16)


def _ffn(x, xb, w_gate, w_up, w_down, ln_g, ln_b, j, l):
    nt = x.shape[0]
    tm, tf = TM_FFN, TF_FFN
    nf = D_FF // tf
    wcol = lambda i, f: (j, 0, jnp.where(i == 0, f, nf - 1))
    wrow = lambda i, f: (j, jnp.where(i == 0, f, nf - 1), 0)
    row = lambda i, f: (i, 0)
    vec = pl.BlockSpec((None, 1, D_MODEL), lambda i, f: (l, 0, 0))
    return pl.pallas_call(
        _ffn_kernel,
        grid=(nt // tm, nf),
        in_specs=[pl.BlockSpec((tm, D_MODEL), row), pl.BlockSpec((tm, D_MODEL), row),
                  pl.BlockSpec((None, D_MODEL, tf), wcol), pl.BlockSpec((None, D_MODEL, tf), wcol),
                  pl.BlockSpec((None, tf, D_MODEL), wrow), vec, vec],
        out_specs=[pl.BlockSpec((tm, D_MODEL), row), pl.BlockSpec((tm, D_MODEL), row)],
        out_shape=[jax.ShapeDtypeStruct((nt, D_MODEL), F32), jax.ShapeDtypeStruct((nt, D_MODEL), BF16)],
        scratch_shapes=[pltpu.VMEM((nf, D_MODEL, tf), BF16), pltpu.VMEM((nf, D_MODEL, tf), BF16),
                        pltpu.VMEM((nf, tf, D_MODEL), BF16), pltpu.VMEM((tm, D_MODEL), F32)],
        compiler_params=_cparams(("arbitrary", "arbitrary")),
        name="ffn",
    )(x, xb, w_gate, w_up, w_down, ln_g.reshape(DEPTH, 1, D_MODEL), ln_b.reshape(DEPTH, 1, D_MODEL))


def _router_kernel(x_ref, w_ref, b_ref, tri_ref, route_ref, cnt_ref, carry_ref):
    @pl.when(pl.program_id(0) == 0)
    def _():
        carry_ref[...] = jnp.zeros_like(carry_ref)

    tm = x_ref.shape[0]
    logits = jnp.dot(x_ref[...], w_ref[...], preferred_element_type=F32, precision=HI) + b_ref[...]
    lane = lax.broadcasted_iota(jnp.int32, (tm, N_EXPERTS), 1).astype(F32)
    m1 = jnp.max(logits, axis=-1, keepdims=True)
    i1 = jnp.min(jnp.where(logits == m1, lane, float(N_EXPERTS)), axis=-1, keepdims=True)
    sel1 = lane == i1
    rest = jnp.where(sel1, -jnp.inf, logits)
    m2 = jnp.max(rest, axis=-1, keepdims=True)
    i2 = jnp.min(jnp.where(rest == m2, lane, float(N_EXPERTS)), axis=-1, keepdims=True)
    sel2 = lane == i2
    e2 = jnp.exp(m2 - m1)
    g1 = 1.0 / (1.0 + e2)
    g2 = e2 / (1.0 + e2)
    sel = (sel1 | sel2).astype(F32)
    rank = jnp.dot(tri_ref[...], sel.astype(BF16), preferred_element_type=F32) + carry_ref[...]
    r1 = jnp.sum(jnp.where(sel1, rank, 0.0), axis=-1, keepdims=True)
    r2 = jnp.sum(jnp.where(sel2, rank, 0.0), axis=-1, keepdims=True)
    cols = (i1, i2, g1, g2, r1, r2)
    route = jnp.zeros((tm, N_EXPERTS), F32)
    for c, val in enumerate(cols):
        route = jnp.where(lane == float(c), val, route)
    route_ref[...] = route
    carry_ref[...] += jnp.sum(sel, axis=0, keepdims=True)
    cnt_ref[...] = carry_ref[...]


def _router(x, w_r, b_r):
    nt = x.shape[0]
    tm = TM_ROW
    tri = (jnp.arange(tm)[:, None] > jnp.arange(tm)[None, :]).astype(BF16)
    return pl.pallas_call(
        _router_kernel,
        grid=(nt // tm,),
        in_specs=[pl.BlockSpec((tm, D_MODEL), lambda i: (i, 0)),
                  pl.BlockSpec((D_MODEL, N_EXPERTS), lambda i: (0, 0)),
                  pl.BlockSpec((1, N_EXPERTS), lambda i: (0, 0)),
                  pl.BlockSpec((tm, tm), lambda i: (0, 0))],
        out_specs=[pl.BlockSpec((tm, N_EXPERTS), lambda i: (i, 0)), pl.BlockSpec((1, N_EXPERTS), lambda i: (0, 0))],
        out_shape=[jax.ShapeDtypeStruct((nt, N_EXPERTS), F32), jax.ShapeDtypeStruct((1, N_EXPERTS), F32)],
        scratch_shapes=[pltpu.VMEM((1, N_EXPERTS), F32)],
        compiler_params=_cparams(("arbitrary",)),
        name="router",
    )(x, w_r, b_r.reshape(1, N_EXPERTS), tri)


def _row_copy(src, dst, sem, s, d):
    return pltpu.make_async_copy(src.at[pl.ds(s, 1)], dst.at[pl.ds(d, 1)], sem)


def _dispatch_kernel(pos_ref, x_ref, zero_ref, xs_ref, sem):
    del zero_ref
    tm = x_ref.shape[0]

    def start(t, carry):
        _row_copy(x_ref, xs_ref, sem, t, pos_ref[0, 2 * t]).start()
        _row_copy(x_ref, xs_ref, sem, t, pos_ref[0, 2 * t + 1]).start()
        return carry

    lax.fori_loop(0, tm, start, 0)

    def wait(t, carry):
        _row_copy(x_ref, xs_ref, sem, 0, 0).wait()
        return carry

    lax.fori_loop(0, 2 * tm, wait, 0)


def _dispatch(x, pos, rows):
    nt = x.shape[0]
    tm = TM_ROW
    zeros = jnp.zeros((rows, D_MODEL), F32)
    return pl.pallas_call(
        _dispatch_kernel,
        grid=(nt // tm,),
        in_specs=[pl.BlockSpec((None, 1, 2 * tm), lambda i: (i, 0, 0), memory_space=pltpu.SMEM),
                  pl.BlockSpec((tm, D_MODEL), lambda i: (i, 0)),
                  pl.BlockSpec(memory_space=pl.ANY)],
        out_specs=pl.BlockSpec(memory_space=pl.ANY),
        out_shape=jax.ShapeDtypeStruct((rows, D_MODEL), F32),
        input_output_aliases={2: 0},
        scratch_shapes=[pltpu.SemaphoreType.DMA(())],
        compiler_params=_cparams(("arbitrary",)),
        name="moe_dispatch",
    )(pos.reshape(nt // tm, 1, 2 * tm), x, zeros)


def _expert_kernel(be_ref, act_ref, x_ref, wg_ref, wu_ref, wd_ref, y_ref, acc_ref):
    j = pl.program_id(0)
    f = pl.program_id(1)

    @pl.when(act_ref[j] == 1)
    def _():
        @pl.when(f == 0)
        def _():
            acc_ref[...] = jnp.zeros_like(acc_ref)

        xb = x_ref[...].astype(BF16)
        hg = jnp.dot(xb, wg_ref[...].astype(BF16), preferred_element_type=F32)
        hu = jnp.dot(xb, wu_ref[...].astype(BF16), preferred_element_type=F32)
        acc_ref[...] += jnp.dot((_silu(hg) * hu).astype(BF16), wd_ref[...].astype(BF16),
                                preferred_element_type=F32)

        @pl.when(f == pl.num_programs(1) - 1)
        def _():
            y_ref[...] = acc_ref[...]

    @pl.when((act_ref[j] == 0) & (f == 0))
    def _():
        y_ref[...] = jnp.zeros_like(y_ref)


def _experts(xs, be, act, w_gate, w_up, w_down, jm):
    rows = xs.shape[0]
    tr, tf = TR_MOE, TF_MOE
    nf = D_FF_E // tf

    def fidx(j, f, act):
        return jnp.where(act[j] == 1, f, nf - 1)

    grid_spec = pltpu.PrefetchScalarGridSpec(
        num_scalar_prefetch=2,
        grid=(rows // tr, nf),
        in_specs=[pl.BlockSpec((tr, D_MODEL), lambda j, f, be, act: (j, 0)),
                  pl.BlockSpec((None, None, D_MODEL, tf), lambda j, f, be, act: (jm, be[j], 0, fidx(j, f, act))),
                  pl.BlockSpec((None, None, D_MODEL, tf), lambda j, f, be, act: (jm, be[j], 0, fidx(j, f, act))),
                  pl.BlockSpec((None, None, tf, D_MODEL), lambda j, f, be, act: (jm, be[j], fidx(j, f, act), 0))],
        out_specs=pl.BlockSpec((tr, D_MODEL), lambda j, f, be, act: (j, 0)),
        scratch_shapes=[pltpu.VMEM((tr, D_MODEL), F32)],
    )
    return pl.pallas_call(
        _expert_kernel,
        grid_spec=grid_spec,
        out_shape=jax.ShapeDtypeStruct((rows, D_MODEL), F32),
        compiler_params=_cparams(("arbitrary", "arbitrary")),
        name="moe_experts",
    )(be, act, xs, w_gate, w_up, w_down)


def _combine_kernel(pos_ref, x_ref, route_ref, g_ref, b_ref, ys_ref, xo_ref, xbo_ref, buf_ref, sem):
    tm = x_ref.shape[0]

    def start(t, carry):
        _row_copy(ys_ref, buf_ref.at[0], sem, pos_ref[0, 2 * t], t).start()
        _row_copy(ys_ref, buf_ref.at[1], sem, pos_ref[0, 2 * t + 1], t).start()
        return carry

    lax.fori_loop(0, tm, start, 0)

    def wait(t, carry):
        _row_copy(ys_ref, buf_ref.at[0], sem, 0, 0).wait()
        return carry

    lax.fori_loop(0, 2 * tm, wait, 0)
    route = route_ref[...]
    y = route[:, 2:3] * buf_ref[0] + route[:, 3:4] * buf_ref[1]
    xn = _ln_rows(ALPHA * x_ref[...] + y, g_ref[...], b_ref[...])
    xo_ref[...] = xn
    xbo_ref[...] = xn.astype(BF16)


def _combine(x, route, pos, ys, ln_g, ln_b, l):
    nt = x.shape[0]
    tm = TM_ROW
    vec = pl.BlockSpec((None, 1, D_MODEL), lambda i: (l, 0, 0))
    row = pl.BlockSpec((tm, D_MODEL), lambda i: (i, 0))
    return pl.pallas_call(
        _combine_kernel,
        grid=(nt // tm,),
        in_specs=[pl.BlockSpec((None, 1, 2 * tm), lambda i: (i, 0, 0), memory_space=pltpu.SMEM),
                  row, pl.BlockSpec((tm, N_EXPERTS), lambda i: (i, 0)), vec, vec,
                  pl.BlockSpec(memory_space=pl.ANY)],
        out_specs=[row, row],
        out_shape=[jax.ShapeDtypeStruct((nt, D_MODEL), F32), jax.ShapeDtypeStruct((nt, D_MODEL), BF16)],
        scratch_shapes=[pltpu.VMEM((2, tm, D_MODEL), F32), pltpu.SemaphoreType.DMA(())],
        compiler_params=_cparams(("arbitrary",)),
        name="moe_combine",
    )(pos.reshape(nt // tm, 1, 2 * tm), x, route, ln_g.reshape(DEPTH, 1, D_MODEL),
      ln_b.reshape(DEPTH, 1, D_MODEL), ys)


def _moe(x, w_r, b_r, w_gate, w_up, w_down, ln_g, ln_b, jm, l):
    nt = x.shape[0]
    tr = TR_MOE
    rows = 2 * nt + N_EXPERTS * tr
    route, cnt = _router(x, w_r[jm], b_r[jm])
    cnt = cnt[0].astype(jnp.int32)
    padded = (cnt + tr - 1) // tr * tr
    ends = jnp.cumsum(padded)
    offs = ends - padded
    i12 = route[:, 0:2].astype(jnp.int32)
    pos = (offs[i12] + route[:, 4:6].astype(jnp.int32)).reshape(-1)
    blk_start = jnp.arange(rows // tr, dtype=jnp.int32) * tr
    act = (blk_start < ends[-1]).astype(jnp.int32)
    be = jnp.sum(blk_start[:, None] >= ends[None, :], axis=1).astype(jnp.int32)
    last = jnp.maximum(ends[-1] // tr - 1, 0)
    be = jnp.where(act == 1, be, be[last])
    xs = _dispatch(x, pos, rows)
    ys = _experts(xs, be, act, w_gate, w_up, w_down, jm)
    return _combine(x, route, pos, ys, ln_g, ln_b, l)


def kernel(x_prompt, x_sample, state_ret, cache_conv, cache_k_w128, cache_v_w128, cache_k_w512, cache_v_w512,
           cache_k_w2048, cache_v_w2048, rel_bias, w_in, w_ret_o, w_conv_dw, b_conv_dw, conv_ln_g, conv_ln_b,
           w_conv_o, w_att_o, w_out, ln1_g, ln1_b, ln2_g, ln2_b, w_ffn_gate, w_ffn_up, w_ffn_down, w_router,
           b_router, w_moe_gate, w_moe_up, w_moe_down):
    nb, seq, _ = x_prompt.shape
    ns = x_sample.shape[0]
    assert x_sample.shape[1] == 1 and seq % 2048 == 0 and ns % 16 == 0
    n_p = nb * seq
    nt = -(-(n_p + ns) // ROW_ALIGN) * ROW_ALIGN
    pads = nt - n_p
    caches = (cache_k_w128, cache_v_w128, cache_k_w512, cache_v_w512, cache_k_w2048, cache_v_w2048)

    x = jnp.concatenate([x_prompt.reshape(n_p, D_MODEL), x_sample.reshape(ns, D_MODEL),
                         jnp.zeros((pads - ns, D_MODEL), F32)], axis=0)
    xb = x.astype(BF16)

    ret_consts = _ret_consts(seq)
    rope_p = _rope_tables(jnp.arange(seq))
    rope_s = _rope_tables(PAST_LEN + jnp.arange(1))

    def tail_rows(small, dtype):
        return jnp.concatenate([small.astype(dtype), jnp.zeros((pads - ns, small.shape[1]), dtype)], axis=0)

    ret_p, ret_s, conv_p, conv_s = [], [], [], []
    kv_p = [[] for _ in range(2 * N_GROUPS)]
    kv_new = [[] for _ in range(2 * N_GROUPS)]
    for l in range(DEPTH):
        h_main = _proj(xb, w_in, l, 0, N_MAIN)
        gates = _proj(xb, w_in, l, COL_GATE, N_GATE)
        hs = h_main[n_p:n_p + ns].astype(F32)

        yr, st_p = _ret_prompt(h_main, nb, seq, ret_consts, rope_p)
        yc, tail_p = _conv_prompt(h_main, nb, seq, w_conv_dw[l], b_conv_dw[l], conv_ln_g[l], conv_ln_b[l])
        yr_s, st_s = _ret_sample(hs, state_ret, ret_consts[4], rope_s, l)
        yc_s, tail_s = _conv_sample(hs, cache_conv[l], w_conv_dw[l], b_conv_dw[l], conv_ln_g[l], conv_ln_b[l])
        ret_p.append(st_p)
        ret_s.append(st_s)
        conv_p.append(tail_p)
        conv_s.append(tail_s)
        mix_p = [yr, yc]
        mix_s = [tail_rows(yr_s, BF16), tail_rows(yc_s, BF16)]
        lse_p, lse_s = [], []
        for g, (win, dil) in enumerate(ATT_GROUPS):
            o, lse = _att_prompt(h_main, rel_bias, g, nb, seq)
            o_s, lse_sg = _att_sample(hs, caches[2 * g], caches[2 * g + 1], rel_bias, g, l)
            mix_p.append(o)
            mix_s.append(tail_rows(o_s, BF16))
            lse_p.append(lse)
            lse_s.append(lse_sg)
            keep = min(win, seq)
            for t in range(2):
                c0 = COL_ATT + (3 * g + 1 + t) * ATT_W
                kv = h_main[:n_p, c0:c0 + ATT_W].reshape(nb, seq, ATT_HEADS, ATT_HD)
                kv_p[2 * g + t].append(kv[:, seq - keep:].astype(F32))
                kv_new[2 * g + t].append(hs[:, c0:c0 + ATT_W].reshape(ns, 1, ATT_HEADS, ATT_HD))
        mix_p.append(jnp.concatenate(lse_p, axis=1))
        mix_s.append(tail_rows(jnp.concatenate(lse_s, axis=1), F32))

        x, xb = _merge(x, mix_p, mix_s, gates, w_ret_o, w_conv_o, w_att_o, w_out, ln1_g, ln1_b, l)
        j = l // 2
        if l % 2 == 0:
            x, xb = _ffn(x, xb, w_ffn_gate, w_ffn_up, w_ffn_down, ln2_g, ln2_b, j, l)
        else:
            x, xb = _moe(x, w_router, b_router, w_moe_gate, w_moe_up, w_moe_down, ln2_g, ln2_b, j, l)

    kv_s = _shift_caches(caches, [jnp.stack(v) for v in kv_new])
    out = [x[:n_p].reshape(nb, seq, D_MODEL), x[n_p:n_p + ns].reshape(ns, 1, D_MODEL),
           jnp.stack(ret_p), jnp.stack(ret_s), jnp.stack(conv_p), jnp.stack(conv_s)]
    for i in range(2 * N_GROUPS):
        out += [jnp.stack(kv_p[i]), kv_s[i]]
    return tuple(out)
```
